```python
import jax
import jax.numpy as jnp
from jax import lax
import numpy as np

D_MODEL = 2048
BATCH = 8
SEQ = 2048
DEPTH = 2
DEC_BATCH = 8
DEC_SEQ = 16
PAST_LEN = 2048

CHUNK = 64
N_MIXERS = 2
N_DELTA_LAYERS = (DEPTH + 1) // 2
N_MLA_LAYERS = DEPTH // 2

DN_K_HEADS = 16
DN_V_HEADS = 32
DN_KEY_DIM = 128
DN_VAL_DIM = 128
DN_CONV_W = 4
DN_Q_W = DN_K_HEADS * DN_KEY_DIM
DN_V_W = DN_V_HEADS * DN_VAL_DIM
DN_CONV_CH = 2 * DN_Q_W + DN_V_W
DN_IN_W = DN_CONV_CH + DN_V_W + 2 * DN_V_HEADS
DELTA_CHUNK = CHUNK

MLA_HEADS = 16
Q_LORA = 512
KV_LORA = 512
NOPE_DIM = 128
ROPE_DIM = 64
V_DIM = 128
QK_DIM = NOPE_DIM + ROPE_DIM
MLA_IN_W = Q_LORA + KV_LORA + ROPE_DIM
ROPE_THETA = 10000.0
Q_BLOCK = 128

N_EXPERTS = 64
N_GROUPS = 8
TOPK_GROUPS = 4
TOP_K = 8
D_EXPERT = 512
ROUTED_SCALE = 2.5
MOE_BLOCK = 128

EPS = 1e-6
F32 = jnp.float32

kernel_name = 'hybrid_deltanet_mla_moe_stream_step'


def rms_norm(x, gain):
    xf = x.astype(F32)
    y = xf * lax.rsqrt(jnp.mean(xf * xf, axis=-1, keepdims=True) + EPS)
    return (y * gain.astype(F32)).astype(x.dtype)


def l2_normalize(x):
    xf = x.astype(F32)
    return xf * lax.rsqrt(jnp.sum(xf * xf, axis=-1, keepdims=True) + EPS)


def adaln_terms(c, w, b):
    mod = jax.nn.silu(c) @ w + b
    return [m[:, None, :] for m in jnp.split(mod, 6, axis=-1)]


def causal_short_conv(u, buf, w):
    L = u.shape[1]
    full = jnp.concatenate([buf.astype(u.dtype), u], axis=1)
    y = full[:, 0:L] * w[0]
    for j in range(1, DN_CONV_W):
        y = y + full[:, j:j + L] * w[j]
    return jax.nn.silu(y), full[:, -(DN_CONV_W - 1):]


def gated_delta_rule(q, k, v, g, beta, s0):
    B, L, H, dk = q.shape
    dv = v.shape[-1]
    C = min(DELTA_CHUNK, L)
    N = L // C

    def blk(t):
        return jnp.moveaxis(t.reshape((B, N, C, H) + t.shape[3:]), 3, 1)

    qb, kb, vb, gb, bb = blk(q), blk(k), blk(v), blk(g), blk(beta)
    gc = jnp.cumsum(gb, axis=-1)
    causal = jnp.tril(jnp.ones((C, C), bool))
    strict = jnp.tril(jnp.ones((C, C), bool), -1)
    decay = jnp.exp(jnp.where(causal, gc[..., :, None] - gc[..., None, :], -jnp.inf))
    k_beta = kb * bb[..., None]
    v_beta = vb * bb[..., None]
    lower = jnp.where(strict, jnp.einsum('bhnid,bhnjd->bhnij', k_beta, kb) * decay, 0.0)
    lower = lower + jnp.eye(C, dtype=F32)
    u = lax.linalg.triangular_solve(lower, v_beta, left_side=True, lower=True, unit_diagonal=True)
    w = lax.linalg.triangular_solve(lower, k_beta * jnp.exp(gc)[..., None], left_side=True,
                                    lower=True, unit_diagonal=True)
    qk = jnp.einsum('bhnid,bhnjd->bhnij', qb, kb) * decay

    def step(S, xs):
        q_n, k_n, u_n, w_n, qk_n, gc_n = xs
        v_new = u_n - jnp.einsum('bhcd,bhde->bhce', w_n, S)
        o = (jnp.einsum('bhcd,bhde->bhce', q_n * jnp.exp(gc_n)[..., None], S)
             + jnp.einsum('bhij,bhje->bhie', qk_n, v_new))
        g_last = gc_n[..., -1]
        k_dec = k_n * jnp.exp(g_last[..., None] - gc_n)[..., None]
        S = S * jnp.exp(g_last)[..., None, None] + jnp.einsum('bhcd,bhce->bhde', k_dec, v_new)
        return S, o

    xs = tuple(jnp.moveaxis(t, 2, 0) for t in (qb, kb, u, w, qk, gc))
    S, o = lax.scan(step, s0, xs)
    o = o.transpose(1, 0, 3, 2, 4).reshape(B, L, H, dv)
    return o, S


def gated_deltanet(h, s0, conv_buf, w_in, conv_w, a_log, dt_bias, out_norm, w_out):
    B, L, _ = h.shape
    proj = h @ w_in
    qkv, z, b, a = jnp.split(proj, [DN_CONV_CH, DN_CONV_CH + DN_V_W,
                                    DN_CONV_CH + DN_V_W + DN_V_HEADS], axis=-1)
    qkv, new_buf = causal_short_conv(qkv, conv_buf, conv_w)
    q, k, v = jnp.split(qkv, [DN_Q_W, 2 * DN_Q_W], axis=-1)
    rep = DN_V_HEADS // DN_K_HEADS
    q = jnp.repeat(q.reshape(B, L, DN_K_HEADS, DN_KEY_DIM), rep, axis=2)
    k = jnp.repeat(k.reshape(B, L, DN_K_HEADS, DN_KEY_DIM), rep, axis=2)
    q = l2_normalize(q) * (DN_KEY_DIM ** -0.5)
    k = l2_normalize(k)
    v = v.reshape(B, L, DN_V_HEADS, DN_VAL_DIM).astype(F32)
    beta = jax.nn.sigmoid(b.astype(F32))
    g = -jnp.exp(a_log.astype(F32)) * jax.nn.softplus(a.astype(F32) + dt_bias.astype(F32))
    o, s_new = gated_delta_rule(q, k, v, g, beta, s0.astype(F32))
    o = rms_norm(o, out_norm) * jax.nn.silu(z.reshape(B, L, DN_V_HEADS, DN_VAL_DIM).astype(F32))
    y = o.reshape(B, L, DN_V_W).astype(h.dtype) @ w_out
    return y, (s_new.astype(s0.dtype), new_buf)


def apply_rope(x, pos):
    half = ROPE_DIM // 2
    inv_freq = ROPE_THETA ** (-jnp.arange(half, dtype=F32) / half)
    ang = pos.astype(F32)[:, None] * inv_freq[None, :]
    ang = ang.reshape((ang.shape[0],) + (1,) * (x.ndim - 3) + (half,))
    cos, sin = jnp.cos(ang), jnp.sin(ang)
    xf = x.astype(F32)
    x1, x2 = xf[..., :half], xf[..., half:]
    return jnp.concatenate([x1 * cos - x2 * sin, x2 * cos + x1 * sin], axis=-1).astype(x.dtype)


def chunk_causal_attention(q, k, v, q_pos, k_pos):
    B, Lq, H, dh = q.shape
    bq = min(Q_BLOCK, Lq)
    nb = Lq // bq
    scale = dh ** -0.5
    k_chunk = k_pos // CHUNK
    qb = q.reshape(B, nb, bq, H, dh).swapaxes(0, 1)
    pb = q_pos.reshape(nb, bq)

    def one_block(args):
        qq, pp = args
        s = jnp.einsum('bqhd,bkhd->bhqk', qq, k, preferred_element_type=F32) * scale
        mask = k_chunk[None, :] <= (pp // CHUNK)[:, None]
        p = jax.nn.softmax(jnp.where(mask[None, None], s, -jnp.inf), axis=-1)
        return jnp.einsum('bhqk,bkhd->bqhd', p.astype(v.dtype), v)

    o = lax.map(one_block, (qb, pb))
    return o.swapaxes(0, 1).reshape(B, Lq, H, v.shape[-1])


def mla_attention(h, past_ckv, past_kpe, w_in, q_norm, kv_norm, wq_b, wkv_b, q_gain, k_gain, w_out):
    B, L, _ = h.shape
    past_len = 0 if past_ckv is None else past_ckv.shape[1]
    pos = past_len + jnp.arange(L)
    cq, ckv, kpe = jnp.split(h @ w_in, [Q_LORA, Q_LORA + KV_LORA], axis=-1)
    q = (rms_norm(cq, q_norm) @ wq_b).reshape(B, L, MLA_HEADS, QK_DIM)
    q = jnp.concatenate([q[..., :NOPE_DIM], apply_rope(q[..., NOPE_DIM:], pos)], axis=-1)
    ckv = rms_norm(ckv, kv_norm)
    kpe = apply_rope(kpe, pos)
    if past_ckv is None:
        all_ckv, all_kpe = ckv, kpe
    else:
        all_ckv = jnp.concatenate([past_ckv.astype(ckv.dtype), ckv], axis=1)
        all_kpe = jnp.concatenate([past_kpe.astype(kpe.dtype), kpe], axis=1)
    S = all_ckv.shape[1]
    k_pos = jnp.arange(S)
    kv = (all_ckv @ wkv_b).reshape(B, S, MLA_HEADS, NOPE_DIM + V_DIM)
    k_nope, v = kv[..., :NOPE_DIM], kv[..., NOPE_DIM:]
    k = jnp.concatenate([k_nope, jnp.broadcast_to(all_kpe[:, :, None, :], (B, S, MLA_HEADS, ROPE_DIM))],
                        axis=-1)
    q = rms_norm(q, q_gain)
    k = rms_norm(k, k_gain)
    o = chunk_causal_attention(q, k, v, pos, k_pos)
    y = o.reshape(B, L, MLA_HEADS * V_DIM) @ w_out
    return y, (ckv, kpe)


def routed_experts(x, top_idx, top_w, w_gate, w_up, w_down):
    T, D = x.shape
    A = T * TOP_K
    e_flat = top_idx.reshape(-1)
    tok_flat = jnp.repeat(jnp.arange(T, dtype=jnp.int32), TOP_K)
    w_flat = top_w.reshape(-1)
    order = jnp.argsort(e_flat)
    e_sorted = e_flat[order]
    counts = jnp.bincount(e_flat, length=N_EXPERTS)
    padded = (counts + MOE_BLOCK - 1) // MOE_BLOCK * MOE_BLOCK
    pad_end = jnp.cumsum(padded)
    pad_start = pad_end - padded
    start = jnp.cumsum(counts) - counts
    dest = pad_start[e_sorted] + jnp.arange(A) - start[e_sorted]
    n_blocks = (A + N_EXPERTS * (MOE_BLOCK - 1) + MOE_BLOCK - 1) // MOE_BLOCK
    R = n_blocks * MOE_BLOCK
    row_tok = jnp.full((R,), T, jnp.int32).at[dest].set(tok_flat[order])
    row_w = jnp.zeros((R,), F32).at[dest].set(w_flat[order])
    block_expert = jnp.minimum(jnp.searchsorted(pad_end, jnp.arange(n_blocks) * MOE_BLOCK, side='right'),
                               N_EXPERTS - 1)
    x_pad = jnp.concatenate([x, jnp.zeros((1, D), x.dtype)], axis=0)

    def expert_block(args):
        tok, e = args
        xb = x_pad[tok]
        return (jax.nn.silu(xb @ w_gate[e]) * (xb @ w_up[e])) @ w_down[e]

    yb = lax.map(expert_block, (row_tok.reshape(n_blocks, MOE_BLOCK), block_expert))
    yb = yb.reshape(R, D) * row_w[:, None].astype(x.dtype)
    return jnp.zeros((T + 1, D), x.dtype).at[row_tok].add(yb)[:T]


def moe_ffn(h, router_w, router_bias, w_gate, w_up, w_down, sw_gate, sw_up, sw_down):
    B, L, D = h.shape
    x = h.reshape(B * L, D)
    T = x.shape[0]
    scores = jax.nn.sigmoid((x @ router_w).astype(F32))
    sel = scores + router_bias.astype(F32)
    grp_score = lax.top_k(sel.reshape(T, N_GROUPS, N_EXPERTS // N_GROUPS), 2)[0].sum(-1)
    top_groups = lax.top_k(grp_score, TOPK_GROUPS)[1]
    gmask = jnp.any(top_groups[:, :, None] == jnp.arange(N_GROUPS)[None, None, :], axis=1)
    sel = jnp.where(jnp.repeat(gmask, N_EXPERTS // N_GROUPS, axis=1), sel, -jnp.inf)
    top_idx = lax.top_k(sel, TOP_K)[1]
    top_w = jnp.take_along_axis(scores, top_idx, axis=1)
    top_w = top_w / jnp.sum(top_w, axis=-1, keepdims=True) * ROUTED_SCALE
    routed = routed_experts(x, top_idx, top_w, w_gate, w_up, w_down)
    shared = (jax.nn.silu(x @ sw_gate) * (x @ sw_up)) @ sw_down
    return (routed + shared).reshape(B, L, D)


def residual_block(x, c, mixer, ada_w, ada_b, norm_mix, norm_ffn, moe_params):
    sh1, sc1, g1, sh2, sc2, g2 = adaln_terms(c, ada_w, ada_b)
    y, state = mixer(rms_norm(x, norm_mix) * (1 + sc1) + sh1)
    x = x + g1 * y
    x = x + g2 * moe_ffn(rms_norm(x, norm_ffn) * (1 + sc2) + sh2, *moe_params)
    return x, state


def setup_inputs(seed: int = 0) -> dict:
    key = jax.random.key(seed)
    keys = list(jax.random.split(key, 48))

    def nrm(shape, scale):
        return jax.random.normal(keys.pop(), shape, F32) * scale

    def gain(shape):
        return 1.0 + nrm(shape, 0.05)

    D = D_MODEL
    nd, nm = N_DELTA_LAYERS, N_MLA_LAYERS
    return {
        'x_prompt': nrm((BATCH, SEQ, D), 1.0),
        'x_sample': nrm((DEC_BATCH, DEC_SEQ, D), 1.0),
        'c_prompt': nrm((BATCH, D), 1.0),
        'c_sample': nrm((DEC_BATCH, D), 1.0),
        'cache_mla_latent': nrm((nm, DEC_BATCH, PAST_LEN, KV_LORA), 1.0),
        'cache_mla_krope': nrm((nm, DEC_BATCH, PAST_LEN, ROPE_DIM), 1.0),
        'state_delta': nrm((nd, DEC_BATCH, DN_V_HEADS, DN_KEY_DIM, DN_VAL_DIM), 0.1),
        'state_delta_conv': nrm((nd, DEC_BATCH, DN_CONV_W - 1, DN_CONV_CH), 1.0),
        'ada_w': nrm((DEPTH, D, 6 * D), 0.5 * D ** -0.5),
        'ada_b': nrm((DEPTH, 6 * D), 0.02),
        'norm_mix': gain((DEPTH, D)),
        'norm_ffn': gain((DEPTH, D)),
        'dn_w_in': nrm((nd, D, DN_IN_W), D ** -0.5),
        'dn_conv_w': nrm((nd, DN_CONV_W, DN_CONV_CH), DN_CONV_W ** -0.5),
        'dn_a_log': jnp.log(jax.random.uniform(keys.pop(), (nd, DN_V_HEADS), F32, 1.0, 16.0)),
        'dn_dt_bias': nrm((nd, DN_V_HEADS), 0.1),
        'dn_out_norm': gain((nd, DN_VAL_DIM)),
        'dn_w_out': nrm((nd, DN_V_W, D), DN_V_W ** -0.5),
        'mla_w_in': nrm((nm, D, MLA_IN_W), D ** -0.5),
        'mla_q_norm': gain((nm, Q_LORA)),
        'mla_kv_norm': gain((nm, KV_LORA)),
        'mla_wq_b': nrm((nm, Q_LORA, MLA_HEADS * QK_DIM), Q_LORA ** -0.5),
        'mla_wkv_b': nrm((nm, KV_LORA, MLA_HEADS * (NOPE_DIM + V_DIM)), KV_LORA ** -0.5),
        'mla_q_gain': gain((nm, QK_DIM)),
        'mla_k_gain': gain((nm, QK_DIM)),
        'mla_w_out': nrm((nm, MLA_HEADS * V_DIM, D), (MLA_HEADS * V_DIM) ** -0.5),
        'moe_router': nrm((DEPTH, D, N_EXPERTS), D ** -0.5),
        'moe_bias': nrm((DEPTH, N_EXPERTS), 0.01),
        'moe_w_gate': nrm((DEPTH, N_EXPERTS, D, D_EXPERT), D ** -0.5),
        'moe_w_up': nrm((DEPTH, N_EXPERTS, D, D_EXPERT), D ** -0.5),
        'moe_w_down': nrm((DEPTH, N_EXPERTS, D_EXPERT, D), D_EXPERT ** -0.5),
        'shared_w_gate': nrm((DEPTH, D, D_EXPERT), D ** -0.5),
        'shared_w_up': nrm((DEPTH, D, D_EXPERT), D ** -0.5),
        'shared_w_down': nrm((DEPTH, D_EXPERT, D), D_EXPERT ** -0.5),
    }


def reference(x_prompt, x_sample, c_prompt, c_sample,
              cache_mla_latent, cache_mla_krope, state_delta, state_delta_conv,
              ada_w, ada_b, norm_mix, norm_ffn,
              dn_w_in, dn_conv_w, dn_a_log, dn_dt_bias, dn_out_norm, dn_w_out,
              mla_w_in, mla_q_norm, mla_kv_norm, mla_wq_b, mla_wkv_b, mla_q_gain, mla_k_gain, mla_w_out,
              moe_router, moe_bias, moe_w_gate, moe_w_up, moe_w_down,
              shared_w_gate, shared_w_up, shared_w_down):
    B = x_prompt.shape[0]
    xp, xs = x_prompt, x_sample
    p_lat, p_kpe, p_dn, p_conv = [], [], [], []
    s_lat, s_kpe, s_dn, s_conv = [], [], [], []
    for i in range(DEPTH):
        lay = (ada_w[i], ada_b[i], norm_mix[i], norm_ffn[i])
        moe = (moe_router[i], moe_bias[i], moe_w_gate[i], moe_w_up[i], moe_w_down[i],
               shared_w_gate[i], shared_w_up[i], shared_w_down[i])
        j = i // N_MIXERS
        if i % N_MIXERS == 0:
            dn = (dn_w_in[j], dn_conv_w[j], dn_a_log[j], dn_dt_bias[j], dn_out_norm[j], dn_w_out[j])
            s0 = jnp.zeros((B, DN_V_HEADS, DN_KEY_DIM, DN_VAL_DIM), xp.dtype)
            b0 = jnp.zeros((B, DN_CONV_W - 1, DN_CONV_CH), xp.dtype)
            xp, (st, bf) = residual_block(xp, c_prompt, lambda h: gated_deltanet(h, s0, b0, *dn), *lay, moe)
            p_dn.append(st)
            p_conv.append(bf)
            xs, (st, bf) = residual_block(
                xs, c_sample, lambda h: gated_deltanet(h, state_delta[j], state_delta_conv[j], *dn), *lay, moe)
            s_dn.append(st)
            s_conv.append(bf)
        else:
            ml = (mla_w_in[j], mla_q_norm[j], mla_kv_norm[j], mla_wq_b[j], mla_wkv_b[j],
                  mla_q_gain[j], mla_k_gain[j], mla_w_out[j])
            xp, (lt, kp) = residual_block(xp, c_prompt, lambda h: mla_attention(h, None, None, *ml), *lay, moe)
            p_lat.append(lt)
            p_kpe.append(kp)
            xs, (lt, kp) = residual_block(
                xs, c_sample, lambda h: mla_attention(h, cache_mla_latent[j], cache_mla_krope[j], *ml),
                *lay, moe)
            s_lat.append(lt)
            s_kpe.append(kp)
    return (xp, xs,
            jnp.stack(p_lat), jnp.stack(p_kpe), jnp.stack(p_dn), jnp.stack(p_conv),
            jnp.stack(s_lat), jnp.stack(s_kpe), jnp.stack(s_dn), jnp.stack(s_conv))
```

```python
import functools

import jax
import jax.numpy as jnp
from jax import lax
from jax.experimental import pallas as pl
from jax.experimental.pallas import tpu as pltpu

F32 = jnp.float32
BF16 = jnp.bfloat16
EPS = 1e-6

V7X_VMEM_LIMIT_BYTES = 56 * 1024 * 1024
LANES = 128

CHUNK = 64
DN_K_HEADS = 16
DN_V_HEADS = 32
DN_HEAD_DIM = 128
DN_CONV_W = 4
MLA_HEADS = 16
Q_LORA = 512
KV_LORA = 512
NOPE_DIM = 128
ROPE_DIM = 64
V_DIM = 128
QK_DIM = NOPE_DIM + ROPE_DIM
HEAD_PAD = 256
ROPE_THETA = 10000.0
N_EXPERTS = 64
N_GROUPS = 8
TOPK_GROUPS = 4
TOP_K = 8
ROUTED_SCALE = 2.5
MOE_TM = 256


def _cparams(*sem):
    return pltpu.CompilerParams(dimension_semantics=sem,
                                vmem_limit_bytes=V7X_VMEM_LIMIT_BYTES)


def _silu(x):
    return x * jax.nn.sigmoid(x)


def _norm_mod(x, gain, scale, shift):
    y = x * lax.rsqrt(jnp.mean(x * x, axis=-1, keepdims=True) + EPS)
    return y * gain * (1.0 + scale) + shift


def _adaln_kernel(c_ref, w_ref, b_ref, o_ref):
    act = _silu(c_ref[...]).astype(BF16)
    o_ref[...] = jnp.dot(act, w_ref[...].astype(BF16),
                         preferred_element_type=F32) + b_ref[...]


def adaln(c, w, b):
    n_seq, d = c.shape
    n = w.shape[1]
    tn = 1024
    return pl.pallas_call(
        _adaln_kernel,
        grid=(n // tn,),
        in_specs=[pl.BlockSpec((n_seq, d), lambda j: (0, 0)),
                  pl.BlockSpec((d, tn), lambda j: (0, j)),
                  pl.BlockSpec((1, tn), lambda j: (0, j))],
        out_specs=pl.BlockSpec((n_seq, tn), lambda j: (0, j)),
        out_shape=jax.ShapeDtypeStruct((n_seq, n), F32),
        compiler_params=_cparams("arbitrary"),
        name="adaln",
    )(c, w, b.reshape(1, n))


def _norm_proj_kernel(*refs, has_side):
    if has_side:
        x_ref, g_ref, sc_ref, sh_ref, w_ref, w2_ref, o_ref, o2_ref, h_scr = refs
    else:
        x_ref, g_ref, sc_ref, sh_ref, w_ref, o_ref, h_scr = refs

    @pl.when(pl.program_id(2) == 0)
    def _():
        h = _norm_mod(x_ref[0], g_ref[...], sc_ref[0], sh_ref[0]).astype(BF16)
        h_scr[...] = h
        if has_side:
            o2_ref[0] = jnp.dot(h, w2_ref[...], preferred_element_type=F32)

    o_ref[0] = jnp.dot(h_scr[...], w_ref[...],
                       preferred_element_type=F32).astype(o_ref.dtype)


def norm_proj(x, gain, scale, shift, w, out_dtype, w_side=None, tn=1024):
    bsz, seq, d = x.shape
    n = w.shape[1]
    tl = min(seq, 1024)
    if n % tn:
        tn = n
    has_side = w_side is not None
    in_specs = [pl.BlockSpec((1, tl, d), lambda b, i, j: (b, i, 0)),
                pl.BlockSpec((1, d), lambda b, i, j: (0, 0)),
                pl.BlockSpec((1, 1, d), lambda b, i, j: (b, 0, 0)),
                pl.BlockSpec((1, 1, d), lambda b, i, j: (b, 0, 0)),
                pl.BlockSpec((d, tn), lambda b, i, j: (0, j))]
    out_specs = [pl.BlockSpec((1, tl, tn), lambda b, i, j: (b, i, j))]
    out_shape = [jax.ShapeDtypeStruct((bsz, seq, n), out_dtype)]
    args = [x, gain.reshape(1, d), scale, shift, w]
    if has_side:
        ns = w_side.shape[1]
        in_specs.append(pl.BlockSpec((d, ns), lambda b, i, j: (0, 0)))
        out_specs.append(pl.BlockSpec((1, tl, ns), lambda b, i, j: (b, i, 0)))
        out_shape.append(jax.ShapeDtypeStruct((bsz, seq, ns), F32))
        args.append(w_side)
    outs = pl.pallas_call(
        functools.partial(_norm_proj_kernel, has_side=has_side),
        grid=(bsz, seq // tl, n // tn),
        in_specs=in_specs, out_specs=out_specs, out_shape=out_shape,
        scratch_shapes=[pltpu.VMEM((tl, d), BF16)],
        compiler_params=_cparams("arbitrary", "arbitrary", "arbitrary"),
        name="norm_proj",
    )(*args)
    return outs if has_side else outs[0]


def _conv_silu(u, buf, w):
    seq = u.shape[0]
    row = lax.broadcasted_iota(jnp.int32, u.shape, 0)
    y = u * w[3:4]
    for s in range(1, DN_CONV_W):
        shifted = pltpu.roll(u, s, 0)
        for r in range(s):
            shifted = jnp.where(row == r, buf[3 - s + r:4 - s + r], shifted)
        y = y + shifted * w[3 - s:4 - s]
    del seq
    return _silu(y)


def _delta_kernel(q_ref, k_ref, v_ref, cbq_ref, cbk_ref, cbv_ref,
                  cwq_ref, cwk_ref, cwv_ref, ba_ref, alog_ref, dtb_ref, s0_ref,
                  o_ref, s_ref, qn_scr, kn_scr, vv_scr, gc_scr, beta_scr,
                  *, valid_len):
    seq = q_ref.shape[1]
    hd = DN_HEAD_DIM
    c = CHUNK
    kh = pl.program_id(1)

    q = _conv_silu(q_ref[0].astype(F32), cbq_ref[0], cwq_ref[...])
    k = _conv_silu(k_ref[0].astype(F32), cbk_ref[0], cwk_ref[...])
    vv_scr[...] = _conv_silu(v_ref[0].astype(F32), cbv_ref[0], cwv_ref[...])
    qn_scr[...] = q * lax.rsqrt(jnp.sum(q * q, axis=-1, keepdims=True) + EPS) * (hd ** -0.5)
    kn_scr[...] = k * lax.rsqrt(jnp.sum(k * k, axis=-1, keepdims=True) + EPS)

    ba = ba_ref[0]
    b_pre = pltpu.roll(ba, LANES - 2 * kh, 1)
    a_pre = pltpu.roll(b_pre, LANES - DN_V_HEADS, 1)
    row = lax.broadcasted_iota(jnp.int32, (seq, LANES), 0)
    valid = row < valid_len
    beta_scr[...] = jnp.where(valid, jax.nn.sigmoid(b_pre), 0.0)
    xa = a_pre + dtb_ref[0]
    softplus = jnp.maximum(xa, 0.0) + jnp.log(1.0 + jnp.exp(-jnp.abs(xa)))
    g = jnp.where(valid, -jnp.exp(alog_ref[0]) * softplus, 0.0)
    pos = row % c
    sft = 1
    while sft < c:
        g = g + jnp.where(pos >= sft, pltpu.roll(g, sft, 0), 0.0)
        sft *= 2
    gc_scr[...] = g

    s_ref[0] = s0_ref[0]

    lane = lax.broadcasted_iota(jnp.int32, (c, LANES), 1)
    sub = lax.broadcasted_iota(jnp.int32, (c, LANES), 0)
    left = lane < c
    col = lane % c
    causal = sub >= col
    strict = sub > col
    eye = jnp.where(sub == col, 1.0, 0.0)

    def wide(colvecs):
        return jnp.where(left, colvecs[:, 0:1], colvecs[:, 1:2])

    def blockdiag(pw):
        return jnp.concatenate([jnp.where(left, pw, 0.0),
                                jnp.where(left, 0.0, pw)], axis=0).astype(BF16)

    def mm(a, b):
        return jnp.dot(a, b, preferred_element_type=F32)

    def chunk_body(n, carry):
        r0 = pl.multiple_of(n * c, c)
        kc = kn_scr[pl.ds(r0, c), :]
        qc = qn_scr[pl.ds(r0, c), :]
        vc = vv_scr[pl.ds(r0, c), :]
        gcv = gc_scr[pl.ds(r0, c), :]
        bcv = beta_scr[pl.ds(r0, c), :]

        gi = wide(gcv)
        stacked = jnp.concatenate([jnp.broadcast_to(gcv[:, 0:1], (c, LANES)),
                                   jnp.broadcast_to(gcv[:, 1:2], (c, LANES))], axis=0)
        gj = stacked.T[0:c, :]
        decay = jnp.exp(jnp.where(causal, gi - gj, -1e30))
        bw = wide(bcv)

        kq = lax.dot_general(jnp.concatenate([kc, qc], axis=0).astype(BF16),
                             jnp.concatenate([kc, kc], axis=0).astype(BF16),
                             (((1,), (1,)), ((), ())), preferred_element_type=F32)
        lower = jnp.where(strict, kq[0:c] * decay * bw, 0.0)
        qk = jnp.where(causal, kq[c:2 * c] * decay, 0.0)

        p = -lower
        t = eye + p
        p = mm(p.astype(BF16), blockdiag(p))
        for _ in range(4):
            r = mm(jnp.concatenate([p, t], axis=0).astype(BF16), blockdiag(p))
            p = r[0:c]
            t = t + r[c:2 * c]
        t = t + mm(t.astype(BF16), blockdiag(p))

        eg = jnp.exp(gcv)
        g_last = gcv[c - 1:c, :]
        e_last = jnp.exp(g_last)
        e_rem = jnp.exp(g_last - gcv)
        for h in range(2):
            t_h = t[:, h * c:(h + 1) * c].astype(BF16)
            qk_h = qk[:, h * c:(h + 1) * c].astype(BF16)
            beta_h = bcv[:, h:h + 1]
            rhs = jnp.concatenate([vc[:, h * hd:(h + 1) * hd] * beta_h,
                                   kc * (beta_h * eg[:, h:h + 1])], axis=1).astype(BF16)
            uw = mm(t_h, rhs)
            s_h = s_ref[0, h]
            wq = jnp.concatenate([uw[:, hd:2 * hd], qc * eg[:, h:h + 1]], axis=0)
            ws = mm(wq.astype(BF16), s_h.astype(BF16))
            v_new = uw[:, 0:hd] - ws[0:c]
            v_new_b = v_new.astype(BF16)
            o_ref[0, pl.ds(r0, c), h * hd:(h + 1) * hd] = (
                ws[c:2 * c] + mm(qk_h, v_new_b)).astype(o_ref.dtype)
            k_dec = (kc * e_rem[:, h:h + 1]).astype(BF16)
            s_ref[0, h] = s_h * e_last[:, h:h + 1] + lax.dot_general(
                k_dec, v_new_b, (((0,), (0,)), ((), ())), preferred_element_type=F32)
        return carry

    lax.fori_loop(0, seq // c, chunk_body, 0)


def delta_core(proj, conv_buf, conv_w, ba, a_log, dt_bias, s0, valid_len):
    bsz, seq, _ = proj.shape
    hd = DN_HEAD_DIM
    nkh = DN_K_HEADS
    q_w = nkh * hd
    alog2 = jnp.pad(a_log.reshape(nkh, 1, 2), ((0, 0), (0, 0), (0, LANES - 2)))
    dtb2 = jnp.pad(dt_bias.reshape(nkh, 1, 2), ((0, 0), (0, 0), (0, LANES - 2)))
    kq_blk = q_w // hd
    v_blk = 2 * q_w // (2 * hd)
    in_specs = [
        pl.BlockSpec((1, seq, hd), lambda b, h: (b, 0, h)),
        pl.BlockSpec((1, seq, hd), lambda b, h: (b, 0, kq_blk + h)),
        pl.BlockSpec((1, seq, 2 * hd), lambda b, h: (b, 0, v_blk + h)),
        pl.BlockSpec((1, 3, hd), lambda b, h: (b, 0, h)),
        pl.BlockSpec((1, 3, hd), lambda b, h: (b, 0, kq_blk + h)),
        pl.BlockSpec((1, 3, 2 * hd), lambda b, h: (b, 0, v_blk + h)),
        pl.BlockSpec((DN_CONV_W, hd), lambda b, h: (0, h)),
        pl.BlockSpec((DN_CONV_W, hd), lambda b, h: (0, kq_blk + h)),
        pl.BlockSpec((DN_CONV_W, 2 * hd), lambda b, h: (0, v_blk + h)),
        pl.BlockSpec((1, seq, LANES), lambda b, h: (b, 0, 0)),
        pl.BlockSpec((1, 1, LANES), lambda b, h: (h, 0, 0)),
        pl.BlockSpec((1, 1, LANES), lambda b, h: (h, 0, 0)),
        pl.BlockSpec((1, 2, hd, hd), lambda b, h: (b, h, 0, 0)),
    ]
    out_specs = [pl.BlockSpec((1, seq, 2 * hd), lambda b, h: (b, 0, h)),
                 pl.BlockSpec((1, 2, hd, hd), lambda b, h: (b, h, 0, 0))]
    out_shape = [jax.ShapeDtypeStruct((bsz, seq, DN_V_HEADS * hd), BF16),
                 jax.ShapeDtypeStruct(s0.shape, F32)]
    return pl.pallas_call(
        functools.partial(_delta_kernel, valid_len=valid_len),
        grid=(bsz, nkh),
        in_specs=in_specs, out_specs=out_specs, out_shape=out_shape,
        scratch_shapes=[pltpu.VMEM((seq, hd), F32), pltpu.VMEM((seq, hd), F32),
                        pltpu.VMEM((seq, 2 * hd), F32),
                        pltpu.VMEM((seq, LANES), F32), pltpu.VMEM((seq, LANES), F32)],
        compiler_params=_cparams("arbitrary", "arbitrary"),
        name="delta_core",
    )(proj, proj, proj, conv_buf, conv_buf, conv_buf, conv_w, conv_w, conv_w,
      ba, alog2, dtb2, s0)


def _out_proj_kernel(*refs, delta_prologue):
    if delta_prologue:
        o_ref, z_ref, gn_ref, w_ref, x_ref, gate_ref, out_ref, lhs_scr = refs

        @pl.when(pl.program_id(2) == 0)
        def _():
            hd = DN_HEAD_DIM
            for h in range(DN_V_HEADS):
                sl = slice(h * hd, (h + 1) * hd)
                oh = o_ref[0, :, sl].astype(F32)
                y = oh * lax.rsqrt(jnp.mean(oh * oh, axis=-1, keepdims=True) + EPS)
                lhs_scr[:, sl] = (y * gn_ref[...] * _silu(z_ref[0, :, sl].astype(F32))).astype(BF16)

        lhs = lhs_scr[...]
    else:
        a_ref, w_ref, x_ref, gate_ref, out_ref = refs
        lhs = a_ref[0]
    y = jnp.dot(lhs, w_ref[...], preferred_element_type=F32)
    out_ref[0] = x_ref[0] + gate_ref[0] * y


def out_proj(lhs, w, x, gate, z_src=None, z_col_block=None, out_norm=None):
    bsz, seq, kdim = lhs.shape
    d = w.shape[1]
    tl = min(seq, 512)
    tn = min(d, 1024)
    delta = z_src is not None
    in_specs = [pl.BlockSpec((1, tl, kdim), lambda b, i, j: (b, i, 0))]
    args = [lhs]
    scratch = []
    if delta:
        in_specs += [pl.BlockSpec((1, tl, kdim), lambda b, i, j: (b, i, z_col_block)),
                     pl.BlockSpec((1, DN_HEAD_DIM), lambda b, i, j: (0, 0))]
        args += [z_src, out_norm.reshape(1, DN_HEAD_DIM)]
        scratch = [pltpu.VMEM((tl, kdim), BF16)]
    in_specs += [pl.BlockSpec((kdim, tn), lambda b, i, j: (0, j)),
                 pl.BlockSpec((1, tl, tn), lambda b, i, j: (b, i, j)),
                 pl.BlockSpec((1, 1, tn), lambda b, i, j: (b, 0, j))]
    args += [w, x, gate]
    return pl.pallas_call(
        functools.partial(_out_proj_kernel, delta_prologue=delta),
        grid=(bsz, seq // tl, d // tn),
        in_specs=in_specs,
        out_specs=pl.BlockSpec((1, tl, tn), lambda b, i, j: (b, i, j)),
        out_shape=jax.ShapeDtypeStruct((bsz, seq, d), F32),
        scratch_shapes=scratch,
        compiler_params=_cparams("arbitrary", "arbitrary", "arbitrary"),
        name="out_proj",
    )(*args)


def _rope_slot(x, cos_t, sin_t):
    half = ROPE_DIM // 2
    lane = lax.broadcasted_iota(jnp.int32, x.shape, 1)
    swapped = jnp.where(lane < half, pltpu.roll(x, LANES - half, 1), pltpu.roll(x, half, 1))
    return x * cos_t + swapped * sin_t


def _mla_q_kernel(c_ref, qn_ref, kvn_ref, wq_ref, qg_ref, cos_ref, sin_ref,
                  q_out, lat_out, kpe_out):
    cq = c_ref[0, :, 0:Q_LORA]
    ckv = c_ref[0, :, Q_LORA:Q_LORA + KV_LORA]
    kpe = c_ref[0, :, Q_LORA + KV_LORA:Q_LORA + KV_LORA + LANES]
    cos_t = cos_ref[...]
    sin_t = sin_ref[...]

    lat_out[0] = ckv * lax.rsqrt(jnp.mean(ckv * ckv, axis=-1, keepdims=True) + EPS) * kvn_ref[...]
    kpe_out[0] = _rope_slot(kpe, cos_t, sin_t)[:, 0:ROPE_DIM]

    cqn = (cq * lax.rsqrt(jnp.mean(cq * cq, axis=-1, keepdims=True) + EPS) * qn_ref[...]).astype(BF16)
    q = jnp.dot(cqn, wq_ref[...], preferred_element_type=F32)
    scale = QK_DIM ** -0.5
    for h in range(MLA_HEADS):
        nope = q[:, h * HEAD_PAD:h * HEAD_PAD + NOPE_DIM]
        rope = _rope_slot(q[:, h * HEAD_PAD + NOPE_DIM:(h + 1) * HEAD_PAD], cos_t, sin_t)
        ms = (jnp.sum(nope * nope, axis=-1, keepdims=True)
              + jnp.sum(rope * rope, axis=-1, keepdims=True)) * (1.0 / QK_DIM)
        rs = lax.rsqrt(ms + EPS) * scale
        q_out[0, :, h * HEAD_PAD:h * HEAD_PAD + NOPE_DIM] = (
            nope * rs * qg_ref[:, 0:NOPE_DIM]).astype(BF16)
        q_out[0, :, h * HEAD_PAD + NOPE_DIM:(h + 1) * HEAD_PAD] = (
            rope * rs * qg_ref[:, NOPE_DIM:HEAD_PAD]).astype(BF16)


def mla_q_path(cqkv, q_norm, kv_norm, wq_p, q_gain_p, cos_t, sin_t):
    bsz, seq, width = cqkv.shape
    tl = min(seq, 512)
    nq = MLA_HEADS * HEAD_PAD
    return pl.pallas_call(
        _mla_q_kernel,
        grid=(bsz, seq // tl),
        in_specs=[pl.BlockSpec((1, tl, width), lambda b, i: (b, i, 0)),
                  pl.BlockSpec((1, Q_LORA), lambda b, i: (0, 0)),
                  pl.BlockSpec((1, KV_LORA), lambda b, i: (0, 0)),
                  pl.BlockSpec((Q_LORA, nq), lambda b, i: (0, 0)),
                  pl.BlockSpec((1, HEAD_PAD), lambda b, i: (0, 0)),
                  pl.BlockSpec((tl, LANES), lambda b, i: (i, 0)),
                  pl.BlockSpec((tl, LANES), lambda b, i: (i, 0))],
        out_specs=[pl.BlockSpec((1, tl, nq), lambda b, i: (b, i, 0)),
                   pl.BlockSpec((1, tl, KV_LORA), lambda b, i: (b, i, 0)),
                   pl.BlockSpec((1, tl, ROPE_DIM), lambda b, i: (b, i, 0))],
        out_shape=[jax.ShapeDtypeStruct((bsz, seq, nq), BF16),
                   jax.ShapeDtypeStruct((bsz, seq, KV_LORA), F32),
                   jax.ShapeDtypeStruct((bsz, seq, ROPE_DIM), F32)],
        compiler_params=_cparams("arbitrary", "arbitrary"),
        name="mla_q_path",
    )(cqkv, q_norm.reshape(1, Q_LORA), kv_norm.reshape(1, KV_LORA), wq_p, q_gain_p,
      cos_t, sin_t)


def _mla_kv_kernel(ckv_ref, kpe_ref, w_ref, kg_ref, k_out, v_out):
    kv = jnp.dot(ckv_ref[0].astype(BF16), w_ref[...], preferred_element_type=F32)
    kpe = kpe_ref[0]
    pe_sq = jnp.sum(kpe * kpe, axis=-1, keepdims=True)
    nv = MLA_HEADS * NOPE_DIM
    v_out[0] = kv[:, nv:nv + MLA_HEADS * V_DIM].astype(BF16)
    for h in range(MLA_HEADS):
        nope = kv[:, h * NOPE_DIM:(h + 1) * NOPE_DIM]
        ms = (jnp.sum(nope * nope, axis=-1, keepdims=True) + pe_sq) * (1.0 / QK_DIM)
        rs = lax.rsqrt(ms + EPS)
        k_out[0, :, h * HEAD_PAD:h * HEAD_PAD + NOPE_DIM] = (
            nope * rs * kg_ref[:, 0:NOPE_DIM]).astype(BF16)
        k_out[0, :, h * HEAD_PAD + NOPE_DIM:(h + 1) * HEAD_PAD] = (
            kpe * rs * kg_ref[:, NOPE_DIM:HEAD_PAD]).astype(BF16)


def mla_kv_path(ckv, kpe_p, wkv_p, k_gain_p):
    bsz, s, _ = ckv.shape
    ts = 256
    nk = MLA_HEADS * HEAD_PAD
    nvv = MLA_HEADS * V_DIM
    nw = wkv_p.shape[1]
    return pl.pallas_call(
        _mla_kv_kernel,
        grid=(bsz, s // ts),
        in_specs=[pl.BlockSpec((1, ts, KV_LORA), lambda b, i: (b, i, 0)),
                  pl.BlockSpec((1, ts, LANES), lambda b, i: (b, i, 0)),
                  pl.BlockSpec((KV_LORA, nw), lambda b, i: (0, 0)),
                  pl.BlockSpec((1, HEAD_PAD), lambda b, i: (0, 0))],
        out_specs=[pl.BlockSpec((1, ts, nk), lambda b, i: (b, i, 0)),
                   pl.BlockSpec((1, ts, nvv), lambda b, i: (b, i, 0))],
        out_shape=[jax.ShapeDtypeStruct((bsz, s, nk), BF16),
                   jax.ShapeDtypeStruct((bsz, s, nvv), BF16)],
        compiler_params=_cparams("arbitrary", "arbitrary"),
        name="mla_kv_path",
    )(ckv, kpe_p, wkv_p, k_gain_p)


def _attn_kernel(q_ref, k_ref, v_ref, o_ref, *, tq, tk, q_pos0, s_valid):
    i = pl.program_id(2)
    q = q_ref[0]
    first_pos = q_pos0 + i * tq
    last_pos = first_pos + tq - 1
    full_extent = jnp.minimum((first_pos // CHUNK + 1) * CHUNK, s_valid)
    any_extent = jnp.minimum((last_pos // CHUNK + 1) * CHUNK, s_valid)
    n_full = full_extent // tk
    n_any = (any_extent + tk - 1) // tk

    q_chunk = (first_pos + lax.broadcasted_iota(jnp.int32, (tq, tk), 0)) // CHUNK
    k_iota = lax.broadcasted_iota(jnp.int32, (tq, tk), 1)

    def step(j, carry, masked):
        m, l, acc = carry
        k0 = pl.multiple_of(j * tk, tk)
        s = lax.dot_general(q, k_ref[0, pl.ds(k0, tk), :], (((1,), (1,)), ((), ())),
                            preferred_element_type=F32)
        if masked:
            kpos = k0 + k_iota
            ok = (kpos // CHUNK <= q_chunk) & (kpos < s_valid)
            s = jnp.where(ok, s, -1e30)
        m_new = jnp.maximum(m, jnp.max(s, axis=-1, keepdims=True))
        alpha = jnp.exp(m - m_new)
        p = jnp.exp(s - m_new)
        l = alpha * l + jnp.sum(p, axis=-1, keepdims=True)
        acc = alpha * acc + jnp.dot(p.astype(BF16), v_ref[0, pl.ds(k0, tk), :],
                                    preferred_element_type=F32)
        return m_new, l, acc

    carry = (jnp.full((tq, 1), -1e30, F32), jnp.zeros((tq, 1), F32),
             jnp.zeros((tq, V_DIM), F32))
    carry = lax.fori_loop(0, n_full, functools.partial(step, masked=False), carry)
    carry = lax.fori_loop(n_full, n_any, functools.partial(step, masked=True), carry)
    _, l, acc = carry
    o_ref[0] = (acc / l).astype(o_ref.dtype)


def chunk_attention(q, k, v, q_pos0, s_valid):
    bsz, lq, _ = q.shape
    s = k.shape[1]
    tq = min(lq, 128)
    tk = 256
    return pl.pallas_call(
        functools.partial(_attn_kernel, tq=tq, tk=tk, q_pos0=q_pos0, s_valid=s_valid),
        grid=(bsz, MLA_HEADS, lq // tq),
        in_specs=[pl.BlockSpec((1, tq, HEAD_PAD), lambda b, h, i: (b, i, h)),
                  pl.BlockSpec((1, s, HEAD_PAD), lambda b, h, i: (b, 0, h)),
                  pl.BlockSpec((1, s, V_DIM), lambda b, h, i: (b, 0, h))],
        out_specs=pl.BlockSpec((1, tq, V_DIM), lambda b, h, i: (b, i, h)),
        out_shape=jax.ShapeDtypeStruct((bsz, lq, MLA_HEADS * V_DIM), BF16),
        compiler_params=_cparams("arbitrary", "arbitrary", "arbitrary"),
        name="chunk_attention",
    )(q, k, v)


def _rank_rows(x, n):
    rows = lax.broadcasted_iota(jnp.int32, x.shape, 0)
    rank = jnp.zeros(x.shape, jnp.int32)
    for r in range(n):
        other = x[r:r + 1, :]
        ahead = (other > x) | ((other == x) & (rows > r))
        rank = rank + jnp.where(ahead, 1, 0)
    return rank


def _moe_front_kernel(x_ref, g_ref, sc_ref, sh_ref, rt_ref, bias_ref, wgu_ref, wd_ref,
                      h_out, ysh_out, w_out, m_out):
    h = _norm_mod(x_ref[0], g_ref[...], sc_ref[0], sh_ref[0])
    h_out[0] = h
    hb = h.astype(BF16)

    logits = lax.dot_general(rt_ref[...], hb, (((1,), (1,)), ((), ())),
                             preferred_element_type=F32)
    scores = jax.nn.sigmoid(logits)
    sel = scores + bias_ref[...]
    gsz = N_EXPERTS // N_GROUPS
    tl = sel.shape[1]
    neg = -jnp.inf
    grp = []
    for gi in range(N_GROUPS):
        blk = sel[gi * gsz:(gi + 1) * gsz, :]
        ridx = lax.broadcasted_iota(jnp.int32, blk.shape, 0)
        m1 = jnp.max(blk, axis=0, keepdims=True)
        first = jnp.min(jnp.where(blk == m1, ridx, gsz), axis=0, keepdims=True)
        m2 = jnp.max(jnp.where(ridx == first, neg, blk), axis=0, keepdims=True)
        grp.append(m1 + m2)
    gscore = jnp.concatenate(grp, axis=0)
    gkeep = _rank_rows(gscore, N_GROUPS) < TOPK_GROUPS
    masked = jnp.concatenate(
        [jnp.where(gkeep[gi:gi + 1, :], sel[gi * gsz:(gi + 1) * gsz, :], neg)
         for gi in range(N_GROUPS)], axis=0)
    chosen = _rank_rows(masked, N_EXPERTS) < TOP_K
    wsel = jnp.where(chosen, scores, 0.0)
    w_out[0] = wsel / jnp.sum(wsel, axis=0, keepdims=True) * ROUTED_SCALE
    m_out[0] = jnp.where(chosen, 1, 0)
    del tl

    f = wd_ref.shape[0]
    gu = jnp.dot(hb, wgu_ref[...], preferred_element_type=F32)
    mid = (_silu(gu[:, 0:f]) * gu[:, f:2 * f]).astype(BF16)
    ysh_out[0] = jnp.dot(mid, wd_ref[...], preferred_element_type=F32)


def moe_front(x, gain, scale, shift, router_t, bias, sw_gu, sw_down):
    bsz, seq, d = x.shape
    tl = min(seq, 512)
    e = router_t.shape[0]
    f = sw_down.shape[0]
    cst = lambda b, i: (0, 0)
    return pl.pallas_call(
        _moe_front_kernel,
        grid=(bsz, seq // tl),
        in_specs=[pl.BlockSpec((1, tl, d), lambda b, i: (b, i, 0)),
                  pl.BlockSpec((1, d), cst),
                  pl.BlockSpec((1, 1, d), lambda b, i: (b, 0, 0)),
                  pl.BlockSpec((1, 1, d), lambda b, i: (b, 0, 0)),
                  pl.BlockSpec((e, d), cst),
                  pl.BlockSpec((e, 1), cst),
                  pl.BlockSpec((d, 2 * f), cst),
                  pl.BlockSpec((f, d), cst)],
        out_specs=[pl.BlockSpec((1, tl, d), lambda b, i: (b, i, 0)),
                   pl.BlockSpec((1, tl, d), lambda b, i: (b, i, 0)),
                   pl.BlockSpec((1, e, tl), lambda b, i: (b, 0, i)),
                   pl.BlockSpec((1, e, tl), lambda b, i: (b, 0, i))],
        out_shape=[jax.ShapeDtypeStruct((bsz, seq, d), F32),
                   jax.ShapeDtypeStruct((bsz, seq, d), F32),
                   jax.ShapeDtypeStruct((bsz, e, seq), F32),
                   jax.ShapeDtypeStruct((bsz, e, seq), jnp.int32)],
        compiler_params=_cparams("arbitrary", "arbitrary"),
        name="moe_front",
    )(x, gain.reshape(1, d), scale, shift, router_t, bias.reshape(e, 1), sw_gu, sw_down)


def _for_each_pow2_piece(count, max_count, fn):
    bit = max_count.bit_length() - 1
    while bit >= 0:
        base = lax.shift_left(lax.shift_right_logical(count, bit + 1), bit + 1)

        @pl.when(lax.shift_right_logical(count, bit) & 1 == 1)
        def _(base=base, size=1 << bit):
            fn(base, size)

        bit -= 1


def _experts_kernel(be_ref, nv_ref, tok_cur, tok_nxt, dst_cur, roww_ref,
                    x_hbm, wg_ref, wu_ref, wd_ref, y_hbm,
                    xbuf, ybuf, wg_b, wu_b, wd_b, sem_in, sem_out):
    i = pl.program_id(0)
    nblk = pl.num_programs(0)
    slot = i % 2
    tm = xbuf.shape[1]
    nv = nv_ref[i]
    nv_next = nv_ref[jnp.minimum(i + 1, nblk - 1)]
    has_next = (i + 1 < nblk) & (nv_next > 0)

    def start_gather(tok_ref, s):
        def body(r, c):
            pltpu.make_async_copy(x_hbm.at[pl.ds(tok_ref[0, 0, r], 1)],
                                  xbuf.at[s, pl.ds(r, 1)], sem_in.at[s]).start()
            return c
        lax.fori_loop(0, tm, body, 0, unroll=8)

    def wait_gather(s):
        pltpu.make_async_copy(x_hbm.at[pl.ds(0, tm)], xbuf.at[s], sem_in.at[s]).wait()

    def start_scatter(s, count):
        def piece(base, size):
            for r in range(size):
                pltpu.make_async_copy(ybuf.at[s, pl.ds(base + r, 1)],
                                      y_hbm.at[pl.ds(dst_cur[0, 0, base + r], 1)],
                                      sem_out.at[s]).start()
        _for_each_pow2_piece(count, tm, piece)

    def wait_scatter(s, count):
        def piece(base, size):
            del base
            pltpu.make_async_copy(ybuf.at[s, pl.ds(0, size)],
                                  y_hbm.at[pl.ds(0, size)], sem_out.at[s]).wait()
        _for_each_pow2_piece(count, tm, piece)

    @pl.when((i == 0) & (nv > 0))
    def _():
        start_gather(tok_cur, 0)

    @pl.when(has_next)
    def _():
        start_gather(tok_nxt, 1 - slot)

    @pl.when(nv > 0)
    def _():
        prev = jnp.maximum(i - 1, 0)

        @pl.when((i == 0) | (be_ref[i] != be_ref[prev]))
        def _():
            wg_b[...] = wg_ref[0].astype(BF16)
            wu_b[...] = wu_ref[0].astype(BF16)
            wd_b[...] = wd_ref[0].astype(BF16)

        wait_gather(slot)

        @pl.when(i >= 2)
        def _():
            wait_scatter(slot, nv_ref[jnp.maximum(i - 2, 0)])

        xb = xbuf[slot].astype(BF16)
        gate = jnp.dot(xb, wg_b[...], preferred_element_type=F32)
        up = jnp.dot(xb, wu_b[...], preferred_element_type=F32)
        mid = (_silu(gate) * up).astype(BF16)
        ybuf[slot] = jnp.dot(mid, wd_b[...], preferred_element_type=F32) * roww_ref[...]
        start_scatter(slot, nv)

        @pl.when(jnp.logical_not(has_next))
        def _():
            wait_scatter(slot, nv)

            @pl.when(i >= 1)
            def _():
                wait_scatter(1 - slot, nv_ref[prev])


def routed_experts(x_tok, be, n_valid, row_tok, row_dst, row_w, w_gate, w_up, w_down, n_out_rows):
    tm = MOE_TM
    nb = be.shape[0]
    d = x_tok.shape[1]
    f = w_gate.shape[2]
    tok3 = row_tok.reshape(nb, 1, tm)
    dst3 = row_dst.reshape(nb, 1, tm)
    smem = pltpu.SMEM
    grid_spec = pltpu.PrefetchScalarGridSpec(
        num_scalar_prefetch=2,
        grid=(nb,),
        in_specs=[
            pl.BlockSpec((1, 1, tm), lambda i, be, nv: (i, 0, 0), memory_space=smem),
            pl.BlockSpec((1, 1, tm), lambda i, be, nv: (jnp.minimum(i + 1, nb - 1), 0, 0),
                         memory_space=smem),
            pl.BlockSpec((1, 1, tm), lambda i, be, nv: (i, 0, 0), memory_space=smem),
            pl.BlockSpec((tm, 1), lambda i, be, nv: (i, 0)),
            pl.BlockSpec(memory_space=pl.ANY),
            pl.BlockSpec((1, d, f), lambda i, be, nv: (be[i], 0, 0)),
            pl.BlockSpec((1, d, f), lambda i, be, nv: (be[i], 0, 0)),
            pl.BlockSpec((1, f, d), lambda i, be, nv: (be[i], 0, 0)),
        ],
        out_specs=pl.BlockSpec(memory_space=pl.ANY),
        scratch_shapes=[pltpu.VMEM((2, tm, d), F32), pltpu.VMEM((2, tm, d), F32),
                        pltpu.VMEM((d, f), BF16), pltpu.VMEM((d, f), BF16),
                        pltpu.VMEM((f, d), BF16),
                        pltpu.SemaphoreType.DMA((2,)), pltpu.SemaphoreType.DMA((2,))],
    )
    return pl.pallas_call(
        _experts_kernel,
        grid_spec=grid_spec,
        out_shape=jax.ShapeDtypeStruct((n_out_rows, d), F32),
        compiler_params=_cparams("arbitrary"),
        name="routed_experts",
    )(be, n_valid, tok3, tok3, dst3, row_w.reshape(nb * tm, 1), x_tok, w_gate, w_up, w_down)


def _combine_kernel(x_ref, gate_ref, ysh_ref, y_ref, o_ref):
    acc = ysh_ref[0]
    for kk in range(TOP_K):
        acc = acc + y_ref[kk]
    o_ref[0] = x_ref[0] + gate_ref[0] * acc


def moe_combine(x, gate, ysh, y_slots, row_block0):
    bsz, seq, d = x.shape
    tt = min(seq, 128)
    nblk = seq // tt
    return pl.pallas_call(
        _combine_kernel,
        grid=(bsz, nblk),
        in_specs=[pl.BlockSpec((1, tt, d), lambda b, i: (b, i, 0)),
                  pl.BlockSpec((1, 1, d), lambda b, i: (b, 0, 0)),
                  pl.BlockSpec((1, tt, d), lambda b, i: (b, i, 0)),
                  pl.BlockSpec((TOP_K, tt, d), lambda b, i: (0, row_block0 + b * nblk + i, 0))],
        out_specs=pl.BlockSpec((1, tt, d), lambda b, i: (b, i, 0)),
        out_shape=jax.ShapeDtypeStruct((bsz, seq, d), F32),
        compiler_params=_cparams("arbitrary", "arbitrary"),
        name="moe_combine",
    )(x, gate, ysh, y_slots)


def _dispatch_tables(mask_et, w_et):
    e, t = mask_et.shape
    a = t * TOP_K
    tm = MOE_TM
    nb = (a + e * (tm - 1) + tm - 1) // tm
    counts = jnp.sum(mask_et, axis=1)
    padded = (counts + tm - 1) // tm * tm
    pad_end = jnp.cumsum(padded)
    pad_start = pad_end - padded
    start = jnp.cumsum(counts) - counts
    flat = jnp.nonzero(mask_et.reshape(-1), size=a, fill_value=0)[0].astype(jnp.int32)
    tok_sorted = flat % t
    slot_et = jnp.cumsum(mask_et, axis=0) - 1
    slot_sorted = slot_et.reshape(-1)[flat]
    w_sorted = w_et.reshape(-1)[flat]

    blk = jnp.arange(nb, dtype=jnp.int32)
    be = jnp.minimum(jnp.searchsorted(pad_end, blk * tm, side='right'), e - 1).astype(jnp.int32)
    n_valid = jnp.clip(counts[be] - (blk * tm - pad_start[be]), 0, tm)
    n_valid = jnp.where(blk * tm < pad_end[-1], n_valid, 0).astype(jnp.int32)
    r = jnp.arange(nb * tm, dtype=jnp.int32)
    er = be[r // tm]
    j = r - pad_start[er]
    valid = (j < counts[er]) & (r < pad_end[-1])
    src = jnp.clip(start[er] + j, 0, a - 1)
    row_tok = jnp.where(valid, tok_sorted[src], 0).astype(jnp.int32)
    row_dst = jnp.where(valid, slot_sorted[src] * t + tok_sorted[src], 0).astype(jnp.int32)
    row_w = jnp.where(valid, w_sorted[src], 0.0)
    return be, n_valid, row_tok, row_dst, row_w


def moe_layer(xs, gates, scales, shifts, gain, router, bias, w_gate, w_up, w_down,
              sw_gate, sw_up, sw_down):
    d = xs[0].shape[-1]
    router_t = router.T.astype(BF16)
    sw_gu = jnp.concatenate([sw_gate, sw_up], axis=1).astype(BF16)
    sw_d = sw_down.astype(BF16)
    fronts = [moe_front(x, gain, sc, sh, router_t, bias, sw_gu, sw_d)
              for x, sc, sh in zip(xs, scales, shifts)]
    h_tok = jnp.concatenate([f[0].reshape(-1, d) for f in fronts], axis=0)
    w_et = jnp.concatenate([jnp.moveaxis(f[2], 1, 0).reshape(N_EXPERTS, -1) for f in fronts], axis=1)
    m_et = jnp.concatenate([jnp.moveaxis(f[3], 1, 0).reshape(N_EXPERTS, -1) for f in fronts], axis=1)
    t = h_tok.shape[0]
    be, n_valid, row_tok, row_dst, row_w = _dispatch_tables(m_et, w_et)
    y = routed_experts(h_tok, be, n_valid, row_tok, row_dst, row_w, w_gate, w_up, w_down,
                       TOP_K * t)
    y_slots = y.reshape(TOP_K, t, d)
    outs = []
    row0 = 0
    for x, gate, f in zip(xs, gates, fronts):
        bsz, seq, _ = x.shape
        tt = min(seq, 128)
        outs.append(moe_combine(x, gate, f[1], y_slots, row0 // tt))
        row0 += bsz * seq
    return outs


def _rope_tables(pos):
    half = ROPE_DIM // 2
    inv_freq = ROPE_THETA ** (-jnp.arange(half, dtype=F32) / half)
    ang = pos.astype(F32)[:, None] * inv_freq[None, :]
    cos, sin = jnp.cos(ang), jnp.sin(ang)
    zeros = jnp.zeros((pos.shape[0], LANES - ROPE_DIM), F32)
    return (jnp.concatenate([cos, cos, zeros], axis=1),
            jnp.concatenate([-sin, sin, zeros], axis=1))


def _pad_head_slots(w, n_heads):
    kdim = w.shape[0]
    w3 = w.reshape(kdim, n_heads, QK_DIM)
    return jnp.pad(w3, ((0, 0), (0, 0), (0, HEAD_PAD - QK_DIM))).reshape(kdim, n_heads * HEAD_PAD)


def deltanet_mixer(x, mods, gain, s0, conv_buf, w_main, w_ba, conv_w, a_log, dt_bias,
                   out_norm, w_out):
    sh1, sc1, g1 = mods
    bsz, seq, _ = x.shape
    conv_ch = conv_w.shape[1]
    proj, ba = norm_proj(x, gain, sc1, sh1, w_main, BF16, w_side=w_ba)
    new_buf = jnp.concatenate([conv_buf, proj[:, :, :conv_ch].astype(F32)],
                              axis=1)[:, -(DN_CONV_W - 1):]
    pad = (-seq) % CHUNK
    if pad:
        proj_p = jnp.pad(proj, ((0, 0), (0, pad), (0, 0)))
        ba_p = jnp.pad(ba, ((0, 0), (0, pad), (0, 0)))
    else:
        proj_p, ba_p = proj, ba
    o, s_new = delta_core(proj_p, conv_buf, conv_w, ba_p, a_log, dt_bias, s0, seq)
    z_blk = conv_ch // (DN_V_HEADS * DN_HEAD_DIM)
    x_new = out_proj(o[:, :seq] if pad else o, w_out, x, g1, z_src=proj, z_col_block=z_blk,
                     out_norm=out_norm)
    return x_new, s_new, new_buf


def mla_mixer(x, mods, gain, past_ckv, past_kpe, w_in_p, q_norm, kv_norm, wq_p, wkv_p,
              q_gain_p, k_gain_p, w_out):
    sh1, sc1, g1 = mods
    bsz, seq, _ = x.shape
    past_len = 0 if past_ckv is None else past_ckv.shape[1]
    pos = past_len + jnp.arange(seq)
    cos_t, sin_t = _rope_tables(pos)
    cqkv = norm_proj(x, gain, sc1, sh1, w_in_p, F32)
    q, latent, kpe = mla_q_path(cqkv, q_norm, kv_norm, wq_p, q_gain_p, cos_t, sin_t)
    if past_ckv is None:
        all_ckv, all_kpe = latent, kpe
    else:
        all_ckv = jnp.concatenate([past_ckv, latent], axis=1)
        all_kpe = jnp.concatenate([past_kpe, kpe], axis=1)
    s_valid = all_ckv.shape[1]
    s_pad = (-s_valid) % 256
    all_ckv = jnp.pad(all_ckv, ((0, 0), (0, s_pad), (0, 0)))
    all_kpe = jnp.pad(all_kpe, ((0, 0), (0, s_pad), (0, LANES - ROPE_DIM)))
    k, v = mla_kv_path(all_ckv, all_kpe, wkv_p, k_gain_p)
    o = chunk_attention(q, k, v, past_len, s_valid)
    x_new = out_proj(o, w_out, x, g1)
    return x_new, latent, kpe


def kernel(x_prompt, x_sample, c_prompt, c_sample, cache_mla_latent, cache_mla_krope, state_delta, state_delta_conv, ada_w, ada_b, norm_mix, norm_ffn, dn_w_in, dn_conv_w, dn_a_log, dn_dt_bias, dn_out_norm, dn_w_out, mla_w_in, mla_q_norm, mla_kv_norm, mla_wq_b, mla_wkv_b, mla_q_gain, mla_k_gain, mla_w_out, moe_router, moe_bias, moe_w_gate, moe_w_up, moe_w_down, shared_w_gate, shared_w_up, shared_w_down):
    depth = ada_w.shape[0]
    n_prompt = x_prompt.shape[0]
    d = x_prompt.shape[-1]
    xs = [x_prompt, x_sample]
    c_all = jnp.concatenate([c_prompt, c_sample], axis=0)
    p_lat, p_kpe, p_dn, p_conv = [], [], [], []
    s_lat, s_kpe, s_dn, s_conv = [], [], [], []

    for layer in range(depth):
        mod = adaln(c_all, ada_w[layer], ada_b[layer])
        mod6 = mod.reshape(c_all.shape[0], 6, 1, d)

        def group_mods(gidx, lo, hi):
            return [mod6[lo:hi, m] for m in gidx]

        mix_mods = [group_mods((0, 1, 2), 0, n_prompt), group_mods((0, 1, 2), n_prompt, None)]
        ffn_mods = [group_mods((3, 4, 5), 0, n_prompt), group_mods((3, 4, 5), n_prompt, None)]
        j = layer // 2
        if layer % 2 == 0:
            conv_ch = dn_conv_w.shape[2]
            z_w = DN_V_HEADS * DN_HEAD_DIM
            w_main = dn_w_in[j][:, :conv_ch + z_w].astype(BF16)
            w_ba = jnp.pad(dn_w_in[j][:, conv_ch + z_w:],
                           ((0, 0), (0, LANES - 2 * DN_V_HEADS))).astype(BF16)
            w_out = dn_w_out[j].astype(BF16)
            zeros_s = jnp.zeros((n_prompt,) + state_delta.shape[2:], F32)
            zeros_b = jnp.zeros((n_prompt,) + state_delta_conv.shape[2:], F32)
            states = [(zeros_s, zeros_b), (state_delta[j], state_delta_conv[j])]
            new_xs = []
            for gi, (x, (s0, b0)) in enumerate(zip(xs, states)):
                x_new, s_new, buf_new = deltanet_mixer(
                    x, mix_mods[gi], norm_mix[layer], s0, b0, w_main, w_ba, dn_conv_w[j],
                    dn_a_log[j], dn_dt_bias[j], dn_out_norm[j], w_out)
                new_xs.append(x_new)
                (p_dn if gi == 0 else s_dn).append(s_new)
                (p_conv if gi == 0 else s_conv).append(buf_new)
            xs = new_xs
        else:
            w_in_p = jnp.pad(mla_w_in[j], ((0, 0), (0, LANES - ROPE_DIM))).astype(BF16)
            wq_p = _pad_head_slots(mla_wq_b[j], MLA_HEADS).astype(BF16)
            wkv3 = mla_wkv_b[j].reshape(KV_LORA, MLA_HEADS, NOPE_DIM + V_DIM)
            wkv_p = jnp.concatenate(
                [wkv3[:, :, :NOPE_DIM].reshape(KV_LORA, -1),
                 wkv3[:, :, NOPE_DIM:].reshape(KV_LORA, -1)], axis=1).astype(BF16)
            q_gain_p = jnp.pad(mla_q_gain[j], (0, HEAD_PAD - QK_DIM)).reshape(1, HEAD_PAD)
            k_gain_p = jnp.pad(mla_k_gain[j], (0, HEAD_PAD - QK_DIM)).reshape(1, HEAD_PAD)
            w_out = mla_w_out[j].astype(BF16)
            pasts = [(None, None), (cache_mla_latent[j], cache_mla_krope[j])]
            new_xs = []
            for gi, (x, (pc, pk)) in enumerate(zip(xs, pasts)):
                x_new, lat, kpe = mla_mixer(
                    x, mix_mods[gi], norm_mix[layer], pc, pk, w_in_p, mla_q_norm[j],
                    mla_kv_norm[j], wq_p, wkv_p, q_gain_p, k_gain_p, w_out)
                new_xs.append(x_new)
                (p_lat if gi == 0 else s_lat).append(lat)
                (p_kpe if gi == 0 else s_kpe).append(kpe)
            xs = new_xs

        xs = moe_layer(xs, [m[2] for m in ffn_mods], [m[1] for m in ffn_mods],
                       [m[0] for m in ffn_mods], norm_ffn[layer], moe_router[layer],
                       moe_bias[layer], moe_w_gate[layer], moe_w_up[layer], moe_w_down[layer],
                       shared_w_gate[layer], shared_w_up[layer], shared_w_down[layer])

    return (xs[0], xs[1],
            jnp.stack(p_lat), jnp.stack(p_kpe), jnp.stack(p_dn), jnp.stack(p_conv),
            jnp.stack(s_lat), jnp.stack(s_kpe), jnp.stack(s_dn), jnp.stack(s_conv))
```
